```python
import math
import jax, jax.numpy as jnp
from jax import lax
import numpy as np

D_MODEL = 1024
BATCH = 16
SEQ = 4096
DEPTH = 1

GDN_HEADS = 8
GDN_DK = 128
GDN_DV = 128
GDN_CONV = 4
GDN_CHUNK = 64
MLA_HEADS = 8
MLA_Q_RANK = 384
MLA_KV_RANK = 256
MLA_NOPE = 128
MLA_ROPE = 64
MLA_V = 128
ROPE_THETA = 10000.0
Q_BLOCK = 128
D_FF = 2816
FFN_CONV = 3
EPS = 1e-6

SPLIT_SIZES = (
    3 * GDN_HEADS * GDN_DK if GDN_DK == GDN_DV else 2 * GDN_HEADS * GDN_DK + GDN_HEADS * GDN_DV,
    GDN_HEADS * GDN_DV,
    GDN_HEADS,
    GDN_HEADS,
    MLA_Q_RANK,
    MLA_KV_RANK,
    MLA_ROPE,
    D_MODEL,
    D_MODEL,
)
D_IN = sum(SPLIT_SIZES)

kernel_name = "hybrid_gdn_mla_convffn_block"


def rmsnorm(x, g):
    xf = x.astype(jnp.float32)
    xf = xf * lax.rsqrt(jnp.mean(xf * xf, axis=-1, keepdims=True) + EPS)
    return (xf * g.astype(jnp.float32)).astype(x.dtype)


def l2norm(x):
    return x * lax.rsqrt(jnp.sum(x * x, axis=-1, keepdims=True) + EPS)


def causal_dwconv(x, w):
    k = w.shape[0]
    return lax.conv_general_dilated(
        x, w[:, None, :].astype(x.dtype), window_strides=(1,), padding=[(k - 1, 0)],
        dimension_numbers=("NWC", "WIO", "NWC"), feature_group_count=x.shape[-1])


def split_in(z):
    offs = list(np.cumsum(SPLIT_SIZES)[:-1])
    return jnp.split(z, [int(o) for o in offs], axis=-1)


def rope(x, pos):
    half = x.shape[-1] // 2
    inv = ROPE_THETA ** (-jnp.arange(half, dtype=jnp.float32) / half)
    ang = pos.astype(jnp.float32)[:, None] * inv[None, :]
    cos = jnp.cos(ang)[:, None, :]
    sin = jnp.sin(ang)[:, None, :]
    xf = x.astype(jnp.float32)
    x1, x2 = xf[..., :half], xf[..., half:]
    return jnp.concatenate([x1 * cos - x2 * sin, x2 * cos + x1 * sin], axis=-1).astype(x.dtype)


def gdn_chunked(q, k, v, g, beta):
    b, s, h, dk = q.shape
    dv = v.shape[-1]
    c = GDN_CHUNK
    n = s // c
    f32 = jnp.float32
    q = l2norm(q.astype(f32)) * (dk ** -0.5)
    k = l2norm(k.astype(f32))
    v = v.astype(f32)

    def chunk(t):
        t = t.reshape((b, n, c, h) + t.shape[3:])
        return jnp.moveaxis(t, 3, 1)

    q, k, v = chunk(q), chunk(k), chunk(v)
    g = jnp.cumsum(chunk(g.astype(f32)), axis=-1)
    beta = chunk(beta.astype(f32))
    k_beta = k * beta[..., None]
    v_beta = v * beta[..., None]

    causal = jnp.tril(jnp.ones((c, c), dtype=bool))
    diff = g[..., :, None] - g[..., None, :]
    decay = jnp.where(causal, jnp.exp(jnp.where(causal, diff, 0.0)), 0.0)

    lmat = jnp.einsum("bhnid,bhnjd->bhnij", k_beta, k) * decay
    rhs = jnp.concatenate([v_beta, k_beta * jnp.exp(g)[..., None]], axis=-1)
    sol = lax.linalg.triangular_solve(lmat, rhs, left_side=True, lower=True, unit_diagonal=True)
    u, w = sol[..., :dv], sol[..., dv:]

    a_intra = jnp.einsum("bhnid,bhnjd->bhnij", q, k) * decay
    q_dec = q * jnp.exp(g)[..., None]
    k_dec = k * jnp.exp(g[..., -1:] - g)[..., None]
    g_last = jnp.exp(g[..., -1])

    xs = tuple(jnp.moveaxis(t, 2, 0) for t in (u, w, q_dec, k_dec, a_intra, g_last))

    def step(state, inp):
        u_n, w_n, qd_n, kd_n, a_n, gl_n = inp
        v_new = u_n - jnp.einsum("bhck,bhkv->bhcv", w_n, state)
        o = jnp.einsum("bhck,bhkv->bhcv", qd_n, state) + jnp.einsum("bhij,bhjv->bhiv", a_n, v_new)
        state = state * gl_n[..., None, None] + jnp.einsum("bhck,bhcv->bhkv", kd_n, v_new)
        return state, o

    s0 = jnp.zeros((b, h, dk, dv), dtype=f32)
    _, o = lax.scan(step, s0, xs)
    o = jnp.transpose(o, (1, 0, 3, 2, 4)).reshape(b, s, h, dv)
    return o


def mla_causal(q_nope, q_pe, k_nope, k_pe, v):
    s = q_nope.shape[1]
    scale = (MLA_NOPE + MLA_ROPE) ** -0.5
    outs = []
    for i in range(s // Q_BLOCK):
        q0, q1 = i * Q_BLOCK, (i + 1) * Q_BLOCK
        sc = (jnp.einsum("bqhd,bkhd->bhqk", q_nope[:, q0:q1], k_nope[:, :q1])
              + jnp.einsum("bqhr,bkr->bhqk", q_pe[:, q0:q1], k_pe[:, :q1]))
        sc = sc.astype(jnp.float32) * scale
        qpos = q0 + jnp.arange(Q_BLOCK)
        kpos = jnp.arange(q1)
        sc = jnp.where(qpos[:, None] >= kpos[None, :], sc, -jnp.inf)
        p = jax.nn.softmax(sc, axis=-1).astype(v.dtype)
        outs.append(jnp.einsum("bhqk,bkhd->bqhd", p, v[:, :q1]))
    return jnp.concatenate(outs, axis=1)


def setup_inputs(seed: int = 0) -> dict:
    key = jax.random.key(seed)
    ks = jax.random.split(key, 24)
    L, D = DEPTH, D_MODEL
    nrm = lambda k, shape, fan: jax.random.normal(k, shape, jnp.float32) * (fan ** -0.5)
    gain = lambda k, n: 1.0 + 0.05 * jax.random.normal(k, (L, n), jnp.float32)
    a_log = jnp.log(jax.random.uniform(ks[3], (L, GDN_HEADS), jnp.float32, 1.0, 16.0))
    dt = jnp.exp(jax.random.uniform(ks[4], (L, GDN_HEADS), jnp.float32, math.log(1e-3), math.log(1e-1)))
    dt_bias = dt + jnp.log(-jnp.expm1(-dt))
    return {
        "x": jax.random.normal(ks[0], (BATCH, SEQ, D), jnp.float32),
        "norm_mix_g": gain(ks[1], D),
        "w_in": nrm(ks[2], (L, D, D_IN), D),
        "conv_qkv_w": nrm(ks[5], (L, GDN_CONV, SPLIT_SIZES[0]), GDN_CONV),
        "gdn_a_log": a_log,
        "gdn_dt_bias": dt_bias,
        "gdn_norm_g": gain(ks[6], GDN_DV),
        "mla_q_norm_g": gain(ks[7], MLA_Q_RANK),
        "w_uq": nrm(ks[8], (L, MLA_Q_RANK, MLA_HEADS * (MLA_NOPE + MLA_ROPE)), MLA_Q_RANK),
        "mla_kv_norm_g": gain(ks[9], MLA_KV_RANK),
        "w_ukv": nrm(ks[10], (L, MLA_KV_RANK, MLA_HEADS * (MLA_NOPE + MLA_V)), MLA_KV_RANK),
        "w_o_gdn": nrm(ks[11], (L, GDN_HEADS * GDN_DV, D), GDN_HEADS * GDN_DV),
        "w_o_mla": nrm(ks[12], (L, MLA_HEADS * MLA_V, D), MLA_HEADS * MLA_V),
        "w_out": nrm(ks[13], (L, D, D), D),
        "norm_ffn_g": gain(ks[14], D),
        "w_up": nrm(ks[15], (L, D, 2 * D_FF), D),
        "conv_ffn_w": nrm(ks[16], (L, FFN_CONV, 2 * D_FF), FFN_CONV),
        "w_down": nrm(ks[17], (L, D_FF, D), D_FF),
        "norm_final_g": 1.0 + 0.05 * jax.random.normal(ks[18], (D,), jnp.float32),
    }


def reference(x, norm_mix_g, w_in, conv_qkv_w, gdn_a_log, gdn_dt_bias, gdn_norm_g,
              mla_q_norm_g, w_uq, mla_kv_norm_g, w_ukv, w_o_gdn, w_o_mla, w_out,
              norm_ffn_g, w_up, conv_ffn_w, w_down, norm_final_g):
    b, s, _ = x.shape
    pos = jnp.arange(s)
    hA, hB = GDN_HEADS, MLA_HEADS
    for l in range(DEPTH):
        h = rmsnorm(x, norm_mix_g[l])
        z = jnp.einsum("bsd,de->bse", h, w_in[l])
        qkv_a, gate_a, a_a, b_a, c_q, c_kv, k_pe, gate_br_a, gate_br_b = split_in(z)

        qkv_a = jax.nn.silu(causal_dwconv(qkv_a, conv_qkv_w[l]))
        q_a, k_a, v_a = jnp.split(qkv_a, [hA * GDN_DK, 2 * hA * GDN_DK], axis=-1)
        q_a = q_a.reshape(b, s, hA, GDN_DK)
        k_a = k_a.reshape(b, s, hA, GDN_DK)
        v_a = v_a.reshape(b, s, hA, GDN_DV)
        g_log = -jnp.exp(gdn_a_log[l].astype(jnp.float32)) * jax.nn.softplus(
            a_a.astype(jnp.float32) + gdn_dt_bias[l].astype(jnp.float32))
        beta = jax.nn.sigmoid(b_a.astype(jnp.float32))
        o_a = gdn_chunked(q_a, k_a, v_a, g_log, beta).astype(x.dtype)
        o_a = rmsnorm(o_a, gdn_norm_g[l]) * jax.nn.silu(gate_a.reshape(b, s, hA, GDN_DV))
        y_a = jnp.einsum("bse,ed->bsd", o_a.reshape(b, s, hA * GDN_DV), w_o_gdn[l])

        cq = rmsnorm(c_q, mla_q_norm_g[l])
        q_b = jnp.einsum("bsr,re->bse", cq, w_uq[l]).reshape(b, s, hB, MLA_NOPE + MLA_ROPE)
        q_nope, q_pe = q_b[..., :MLA_NOPE], rope(q_b[..., MLA_NOPE:], pos)
        ckv = rmsnorm(c_kv, mla_kv_norm_g[l])
        kv = jnp.einsum("bsr,re->bse", ckv, w_ukv[l]).reshape(b, s, hB, MLA_NOPE + MLA_V)
        k_nope, v_b = kv[..., :MLA_NOPE], kv[..., MLA_NOPE:]
        k_pe_r = rope(k_pe[:, :, None, :], pos)[:, :, 0, :]
        o_b = mla_causal(q_nope, q_pe, k_nope, k_pe_r, v_b)
        y_b = jnp.einsum("bse,ed->bsd", o_b.reshape(b, s, hB * MLA_V), w_o_mla[l])

        merged = jax.nn.sigmoid(gate_br_a) * y_a + jax.nn.sigmoid(gate_br_b) * y_b
        x = x + jnp.einsum("bsd,de->bse", merged, w_out[l])

        h = rmsnorm(x, norm_ffn_g[l])
        u = causal_dwconv(jnp.einsum("bsd,df->bsf", h, w_up[l]), conv_ffn_w[l])
        gate_f, up_f = u[..., :D_FF], u[..., D_FF:]
        x = x + jnp.einsum("bsf,fd->bsd", jax.nn.silu(gate_f) * up_f, w_down[l])
    return rmsnorm(x, norm_final_g)
```

```python
import functools
import math

import jax
import jax.numpy as jnp
from jax import lax
from jax.experimental import pallas as pl
from jax.experimental.pallas import tpu as pltpu

F32 = jnp.float32
BF16 = jnp.bfloat16

D_MODEL = 1024
HEADS = 8
HEAD_DIM = 128
GDN_CONV = 4
GDN_CHUNK = 64
MLA_Q_RANK = 384
MLA_KV_RANK = 256
MLA_ROPE = 64
ROPE_THETA = 10000.0
D_FF = 2816
FFN_CONV = 3
EPS = 1e-6

QK_PAD = 256
NZ = 6912
Z_Q, Z_K, Z_V, Z_GATE, Z_GBA, Z_GBB, Z_CQ, Z_KPE, Z_CKV = 0, 1024, 2048, 3072, 4096, 5120, 6144, 6528, 6656

VMEM_LIMIT = 56 * 1024 * 1024


def _mm(a, b):
    return jnp.dot(a.astype(BF16), b.astype(BF16), preferred_element_type=F32)


def _mm_nt(a, b):
    return lax.dot_general(a.astype(BF16), b.astype(BF16), (((1,), (1,)), ((), ())),
                           preferred_element_type=F32)


def _sigmoid(x):
    return 1.0 / (1.0 + jnp.exp(-x))


def _silu(x):
    return x * _sigmoid(x)


def _softplus(x):
    return jnp.maximum(x, 0.0) + jnp.log(1.0 + jnp.exp(-jnp.abs(x)))


def _inproj_kernel(x_ref, g_ref, w_ref, wab_ref, nega_ref, dtb_ref, z_ref, gab_ref, h_ref, *, tm):
    j = pl.program_id(1)

    @pl.when(j == 0)
    def _():
        x = x_ref[...]
        h = x * lax.rsqrt(jnp.mean(x * x, axis=-1, keepdims=True) + EPS) * g_ref[...]
        hb = h.astype(BF16)
        h_ref[...] = hb
        ab = jnp.dot(hb, wab_ref[...], preferred_element_type=F32)
        glog = nega_ref[...] * _softplus(ab + dtb_ref[...])
        beta = _sigmoid(ab)
        sub = 256
        r = lax.broadcasted_iota(jnp.int32, (sub, sub), 0)
        c = lax.broadcasted_iota(jnp.int32, (sub, sub), 1)
        same = (r // GDN_CHUNK) == (c // GDN_CHUNK)
        tri = jnp.where(same & (r >= c), 1.0, 0.0).astype(F32)
        blk = jnp.where(same, 1.0, 0.0).astype(F32)
        lane = lax.broadcasted_iota(jnp.int32, (sub, 128), 1)
        for sb in range(tm // sub):
            rows = slice(sb * sub, (sb + 1) * sub)
            gl = glog[rows]
            gc = jnp.dot(tri, gl, precision=lax.Precision.HIGHEST, preferred_element_type=F32)
            gt = jnp.dot(blk, gl, precision=lax.Precision.HIGHEST, preferred_element_type=F32)
            gab_ref[rows, :] = jnp.where(lane < 8, gc, jnp.where(lane < 16, beta[rows], gt))

    z_ref[...] = jnp.dot(h_ref[...], w_ref[...], preferred_element_type=F32).astype(BF16)


def _inproj(x2, g, wz, wab, nega, dtb, *, tm, tn):
    t = x2.shape[0]
    return pl.pallas_call(
        functools.partial(_inproj_kernel, tm=tm),
        grid=(t // tm, NZ // tn),
        in_specs=[
            pl.BlockSpec((tm, D_MODEL), lambda i, j: (i, 0)),
            pl.BlockSpec((1, D_MODEL), lambda i, j: (0, 0)),
            pl.BlockSpec((D_MODEL, tn), lambda i, j: (0, j)),
            pl.BlockSpec((D_MODEL, 128), lambda i, j: (0, 0)),
            pl.BlockSpec((1, 128), lambda i, j: (0, 0)),
            pl.BlockSpec((1, 128), lambda i, j: (0, 0)),
        ],
        out_specs=[
            pl.BlockSpec((tm, tn), lambda i, j: (i, j)),
            pl.BlockSpec((tm, 128), lambda i, j: (i, 0)),
        ],
        out_shape=[
            jax.ShapeDtypeStruct((t, NZ), BF16),
            jax.ShapeDtypeStruct((t, 128), F32),
        ],
        scratch_shapes=[pltpu.VMEM((tm, D_MODEL), BF16)],
        compiler_params=pltpu.CompilerParams(
            dimension_semantics=("parallel", "arbitrary"), vmem_limit_bytes=VMEM_LIMIT),
        name="inproj",
    )(x2, g, wz, wab, nega, dtb)


GDN_ROWS = 128


def _gdn_kernel(q_ref, k_ref, v_ref, gate_ref, gab_ref, cw_ref, ng_ref, out_ref, halo_ref, state_ref):
    R = GDN_ROWS
    C = GDN_CHUNK

    @pl.when(pl.program_id(1) == 0)
    def _():
        halo_ref[...] = jnp.zeros_like(halo_ref)
        state_ref[...] = jnp.zeros_like(state_ref)

    def conv_silu(x_ref, idx):
        x = x_ref[...].astype(F32)
        xe = jnp.concatenate([halo_ref[idx], x], axis=0)
        w = cw_ref[idx]
        y = xe[5:5 + R] * w[0:1] + xe[6:6 + R] * w[1:2] + xe[7:7 + R] * w[2:3] + x * w[3:4]
        halo_ref[idx] = x[R - 8:R]
        return _silu(y)

    qs = conv_silu(q_ref, 0)
    ks = conv_silu(k_ref, 1)
    vs = conv_silu(v_ref, 2)

    gab = gab_ref[...]
    gab_t = gab.T

    ii = lax.broadcasted_iota(jnp.int32, (R, R), 0)
    jj = lax.broadcasted_iota(jnp.int32, (R, R), 1)
    same64 = (ii // C) == (jj // C)
    blk_causal = same64 & (ii >= jj)
    blk_strict = same64 & (ii > jj)
    m16 = (ii // 16) == (jj // 16)
    m32 = (ii // 32) == (jj // 32)
    eye = jnp.where(ii == jj, 1.0, 0.0).astype(F32)
    zeros_c = jnp.zeros((C, HEAD_DIM), F32)

    for h in range(HEADS):
        sl = slice(h * HEAD_DIM, (h + 1) * HEAD_DIM)
        q = qs[:, sl]
        k = ks[:, sl]
        v = vs[:, sl]
        q = q * (lax.rsqrt(jnp.sum(q * q, axis=-1, keepdims=True) + EPS) * (HEAD_DIM ** -0.5))
        k = k * lax.rsqrt(jnp.sum(k * k, axis=-1, keepdims=True) + EPS)
        gcol = gab[:, h:h + 1]
        beta = gab[:, 8 + h:9 + h]
        glast = gab[:, 16 + h:17 + h]
        grow = gab_t[h:h + 1, :]
        eg = jnp.exp(gcol)
        kb = k * beta
        vb = v * beta
        kbg = kb * eg
        qd = q * eg
        kd = k * jnp.exp(glast - gcol)

        diff = gcol - grow
        dec = jnp.where(blk_causal, jnp.exp(jnp.where(blk_causal, diff, 0.0)), 0.0)
        a = _mm_nt(jnp.concatenate([kb, q], axis=0), k)
        lmat = jnp.where(blk_strict, a[:R] * dec, 0.0)
        aqk = a[R:] * dec

        l16 = jnp.where(m16, lmat, 0.0)
        t = eye - l16
        p = _mm(l16, l16)
        t = t + _mm(t, p)
        p = _mm(p, p)
        t = t + _mm(t, p)
        p = _mm(p, p)
        t = t + _mm(t, p)
        c32 = jnp.where(m32 & jnp.logical_not(m16), lmat, 0.0)
        t = t - _mm(t, _mm(c32, t))
        c64 = jnp.where(jnp.logical_not(m32), lmat, 0.0)
        t = t - _mm(t, _mm(c64, t))

        uw = _mm(t, jnp.concatenate([vb, kbg], axis=1))
        u = uw[:, :HEAD_DIM]
        w = uw[:, HEAD_DIM:]
        kd_t = kd.T

        s = state_ref[h]
        ws = _mm(jnp.concatenate([w[:C], qd[:C]], axis=0), s)
        vn0 = u[:C] - ws[:C]
        oi0 = ws[C:]
        s = s * jnp.exp(glast[0:1]) + _mm(kd_t, jnp.concatenate([vn0, zeros_c], axis=0))
        ws = _mm(jnp.concatenate([w[C:], qd[C:]], axis=0), s)
        vn1 = u[C:] - ws[:C]
        oi1 = ws[C:]
        s = s * jnp.exp(glast[C:C + 1]) + _mm(kd_t, jnp.concatenate([zeros_c, vn1], axis=0))
        state_ref[h] = s

        o = jnp.concatenate([oi0, oi1], axis=0) + _mm(aqk, jnp.concatenate([vn0, vn1], axis=0))
        o = o * lax.rsqrt(jnp.mean(o * o, axis=-1, keepdims=True) + EPS) * ng_ref[...]
        out_ref[:, sl] = (o * _silu(gate_ref[:, sl].astype(F32))).astype(BF16)


def _gdn(z, gab, cw, ng, *, batch, seq):
    R = GDN_ROWS
    ns = seq // R
    t = batch * seq
    zspec = lambda col: pl.BlockSpec((R, D_MODEL), lambda b, s: (b * ns + s, col))
    return pl.pallas_call(
        _gdn_kernel,
        grid=(batch, ns),
        in_specs=[
            zspec(Z_Q // D_MODEL), zspec(Z_K // D_MODEL), zspec(Z_V // D_MODEL), zspec(Z_GATE // D_MODEL),
            pl.BlockSpec((R, 128), lambda b, s: (b * ns + s, 0)),
            pl.BlockSpec((3, GDN_CONV, D_MODEL), lambda b, s: (0, 0, 0)),
            pl.BlockSpec((1, HEAD_DIM), lambda b, s: (0, 0)),
        ],
        out_specs=pl.BlockSpec((R, D_MODEL), lambda b, s: (b * ns + s, 0)),
        out_shape=jax.ShapeDtypeStruct((t, D_MODEL), BF16),
        scratch_shapes=[
            pltpu.VMEM((3, 8, D_MODEL), F32),
            pltpu.VMEM((HEADS, HEAD_DIM, HEAD_DIM), F32),
        ],
        compiler_params=pltpu.CompilerParams(
            dimension_semantics=("parallel", "arbitrary"), vmem_limit_bytes=VMEM_LIMIT),
        name="gdn",
    )(z, z, z, z, gab, cw, ng)


def _mlaproj_kernel(cq_ref, kpe_ref, ckv_ref, gq_ref, gkv_ref, wq_ref, wkv_ref, cos_ref, sin_ref,
                    q_ref, k_ref, v_ref, *, scale):
    def rms(x, g):
        return x * lax.rsqrt(jnp.mean(x * x, axis=-1, keepdims=True) + EPS) * g

    cq = rms(cq_ref[...].astype(F32), gq_ref[...])
    ckv = rms(ckv_ref[...].astype(F32), gkv_ref[...])
    qall = _mm(cq, wq_ref[...]) * scale
    kvall = _mm(ckv, wkv_ref[...])
    cos = cos_ref[...]
    sin = sin_ref[...]

    def rot(grp):
        return grp * cos + pltpu.roll(grp, 64, axis=1) * sin

    kpe = rot(kpe_ref[...].astype(F32)).astype(BF16)
    for h in range(HEADS):
        b0 = h * 2 * HEAD_DIM
        q_ref[0, h, :, :HEAD_DIM] = qall[:, b0:b0 + HEAD_DIM].astype(BF16)
        q_ref[0, h, :, HEAD_DIM:] = rot(qall[:, b0 + HEAD_DIM:b0 + 2 * HEAD_DIM]).astype(BF16)
        k_ref[0, h, :, :HEAD_DIM] = kvall[:, b0:b0 + HEAD_DIM].astype(BF16)
        k_ref[0, h, :, HEAD_DIM:] = kpe
        v_ref[0, h] = kvall[:, b0 + HEAD_DIM:b0 + 2 * HEAD_DIM].astype(BF16)


def _mlaproj(z, gq, gkv, wq, wkv, cos, sin, *, batch, seq, tm):
    ns = seq // tm
    scale = (HEAD_DIM + MLA_ROPE) ** -0.5
    return pl.pallas_call(
        functools.partial(_mlaproj_kernel, scale=scale),
        grid=(batch, ns),
        in_specs=[
            pl.BlockSpec((tm, MLA_Q_RANK), lambda b, s: (b * ns + s, Z_CQ // MLA_Q_RANK)),
            pl.BlockSpec((tm, 128), lambda b, s: (b * ns + s, Z_KPE // 128)),
            pl.BlockSpec((tm, MLA_KV_RANK), lambda b, s: (b * ns + s, Z_CKV // MLA_KV_RANK)),
            pl.BlockSpec((1, MLA_Q_RANK), lambda b, s: (0, 0)),
            pl.BlockSpec((1, MLA_KV_RANK), lambda b, s: (0, 0)),
            pl.BlockSpec((MLA_Q_RANK, HEADS * QK_PAD), lambda b, s: (0, 0)),
            pl.BlockSpec((MLA_KV_RANK, HEADS * 2 * HEAD_DIM), lambda b, s: (0, 0)),
            pl.BlockSpec((tm, 128), lambda b, s: (s, 0)),
            pl.BlockSpec((tm, 128), lambda b, s: (s, 0)),
        ],
        out_specs=[
            pl.BlockSpec((1, HEADS, tm, QK_PAD), lambda b, s: (b, 0, s, 0)),
            pl.BlockSpec((1, HEADS, tm, QK_PAD), lambda b, s: (b, 0, s, 0)),
            pl.BlockSpec((1, HEADS, tm, HEAD_DIM), lambda b, s: (b, 0, s, 0)),
        ],
        out_shape=[
            jax.ShapeDtypeStruct((batch, HEADS, seq, QK_PAD), BF16),
            jax.ShapeDtypeStruct((batch, HEADS, seq, QK_PAD), BF16),
            jax.ShapeDtypeStruct((batch, HEADS, seq, HEAD_DIM), BF16),
        ],
        compiler_params=pltpu.CompilerParams(
            dimension_semantics=("parallel", "parallel"), vmem_limit_bytes=VMEM_LIMIT),
        name="mlaproj",
    )(z, z, z, gq, gkv, wq, wkv, cos, sin)


def _attn_kernel(q_ref, k_ref, v_ref, o_ref, m_ref, l_ref, acc_ref, *, tq):
    qi = pl.program_id(2)
    q = q_ref[0, 0]
    m_ref[...] = jnp.full_like(m_ref, -jnp.inf)
    l_ref[...] = jnp.zeros_like(l_ref)
    acc_ref[...] = jnp.zeros_like(acc_ref)

    def block(ki, masked):
        start = pl.multiple_of(ki * tq, tq)
        k = k_ref[0, 0, pl.ds(start, tq), :]
        v = v_ref[0, 0, pl.ds(start, tq), :]
        s = lax.dot_general(q, k, (((1,), (1,)), ((), ())), preferred_element_type=F32)
        if masked:
            r = lax.broadcasted_iota(jnp.int32, (tq, tq), 0)
            c = lax.broadcasted_iota(jnp.int32, (tq, tq), 1)
            s = jnp.where(r >= c, s, -jnp.inf)
        m_old = m_ref[...]
        m_new = jnp.maximum(m_old, jnp.max(s, axis=-1, keepdims=True))
        alpha = jnp.exp(m_old - m_new)
        p = jnp.exp(s - m_new)
        l_ref[...] = alpha * l_ref[...] + jnp.sum(p, axis=-1, keepdims=True)
        acc_ref[...] = alpha * acc_ref[...] + jnp.dot(p.astype(BF16), v, preferred_element_type=F32)
        m_ref[...] = m_new

    def body(ki, carry):
        block(ki, False)
        return carry

    lax.fori_loop(0, qi, body, 0)
    block(qi, True)
    o_ref[0] = (acc_ref[...] / l_ref[...]).astype(o_ref.dtype)


def _attn(q, k, v, *, tq):
    batch, heads, seq, _ = q.shape
    nq = seq // tq
    return pl.pallas_call(
        functools.partial(_attn_kernel, tq=tq),
        grid=(batch, heads, nq),
        in_specs=[
            pl.BlockSpec((1, 1, tq, QK_PAD), lambda b, h, i: (b, h, i, 0)),
            pl.BlockSpec((1, 1, seq, QK_PAD), lambda b, h, i: (b, h, 0, 0)),
            pl.BlockSpec((1, 1, seq, HEAD_DIM), lambda b, h, i: (b, h, 0, 0)),
        ],
        out_specs=pl.BlockSpec((1, tq, HEAD_DIM), lambda b, h, i: (b, i, h)),
        out_shape=jax.ShapeDtypeStruct((batch, seq, heads * HEAD_DIM), BF16),
        scratch_shapes=[
            pltpu.VMEM((tq, 1), F32),
            pltpu.VMEM((tq, 1), F32),
            pltpu.VMEM((tq, HEAD_DIM), F32),
        ],
        compiler_params=pltpu.CompilerParams(
            dimension_semantics=("parallel", "parallel", "arbitrary"), vmem_limit_bytes=VMEM_LIMIT),
        name="attn",
    )(q, k, v)


def _mix_kernel(oa_ref, ob_ref, ga_ref, gb_ref, x_ref, woa_ref, wob_ref, wout_ref, o_ref):
    ya = jnp.dot(oa_ref[...], woa_ref[...], preferred_element_type=F32)
    yb = jnp.dot(ob_ref[...], wob_ref[...], preferred_element_type=F32)
    merged = _sigmoid(ga_ref[...].astype(F32)) * ya + _sigmoid(gb_ref[...].astype(F32)) * yb
    o_ref[...] = x_ref[...] + jnp.dot(merged.astype(BF16), wout_ref[...], preferred_element_type=F32)


def _mix(oa, ob, z, x2, woa, wob, wout, *, tm):
    t = x2.shape[0]
    row = lambda col: pl.BlockSpec((tm, D_MODEL), lambda i: (i, col))
    wspec = pl.BlockSpec((D_MODEL, D_MODEL), lambda i: (0, 0))
    return pl.pallas_call(
        _mix_kernel,
        grid=(t // tm,),
        in_specs=[row(0), row(0), row(Z_GBA // D_MODEL), row(Z_GBB // D_MODEL), row(0), wspec, wspec, wspec],
        out_specs=row(0),
        out_shape=jax.ShapeDtypeStruct((t, D_MODEL), F32),
        compiler_params=pltpu.CompilerParams(
            dimension_semantics=("parallel",), vmem_limit_bytes=VMEM_LIMIT),
        name="mix",
    )(oa, ob, z, z, x2, woa, wob, wout)


def _ffn_kernel(xh_ref, x_ref, g_ref, wg_ref, wu_ref, cg_ref, cu_ref, wd_ref, gf_ref, o_ref,
                h_ref, acc_ref, *, tm, seq, final_norm):
    i = pl.program_id(0)
    j = pl.program_id(1)

    @pl.when(j == 0)
    def _():
        xe = jnp.concatenate([xh_ref[...], x_ref[...]], axis=0)
        h = xe * lax.rsqrt(jnp.mean(xe * xe, axis=-1, keepdims=True) + EPS) * g_ref[...]
        h_ref[...] = h.astype(BF16)
        acc_ref[...] = jnp.zeros_like(acc_ref)

    h = h_ref[...]
    keep = jnp.where((i * tm) % seq == 0, 0.0, 1.0)
    rowid = lax.broadcasted_iota(jnp.int32, (tm + 8, 1), 0)
    halo_scale = jnp.where(rowid < 8, keep, 1.0)

    def conv(w_ref, c_ref):
        ue = jnp.dot(h, w_ref[...], preferred_element_type=F32) * halo_scale
        cw = c_ref[...]
        return ue[6:6 + tm] * cw[0:1] + ue[7:7 + tm] * cw[1:2] + ue[8:8 + tm] * cw[2:3]

    act = _silu(conv(wg_ref, cg_ref)) * conv(wu_ref, cu_ref)
    acc_ref[...] += jnp.dot(act.astype(BF16), wd_ref[...], preferred_element_type=F32)

    @pl.when(j == pl.num_programs(1) - 1)
    def _():
        y = x_ref[...] + acc_ref[...]
        if final_norm:
            y = y * lax.rsqrt(jnp.mean(y * y, axis=-1, keepdims=True) + EPS) * gf_ref[...]
        o_ref[...] = y


def _ffn(x1, g, wup, cw, wd, gf, *, seq, tm, tf, final_norm):
    t = x1.shape[0]
    nf = D_FF // tf
    return pl.pallas_call(
        functools.partial(_ffn_kernel, tm=tm, seq=seq, final_norm=final_norm),
        grid=(t // tm, nf),
        in_specs=[
            pl.BlockSpec((8, D_MODEL), lambda i, j: (jnp.maximum(i * (tm // 8) - 1, 0), 0)),
            pl.BlockSpec((tm, D_MODEL), lambda i, j: (i, 0)),
            pl.BlockSpec((1, D_MODEL), lambda i, j: (0, 0)),
            pl.BlockSpec((D_MODEL, tf), lambda i, j: (0, j)),
            pl.BlockSpec((D_MODEL, tf), lambda i, j: (0, nf + j)),
            pl.BlockSpec((FFN_CONV, tf), lambda i, j: (0, j)),
            pl.BlockSpec((FFN_CONV, tf), lambda i, j: (0, nf + j)),
            pl.BlockSpec((tf, D_MODEL), lambda i, j: (j, 0)),
            pl.BlockSpec((1, D_MODEL), lambda i, j: (0, 0)),
        ],
        out_specs=pl.BlockSpec((tm, D_MODEL), lambda i, j: (i, 0)),
        out_shape=jax.ShapeDtypeStruct((t, D_MODEL), F32),
        scratch_shapes=[
            pltpu.VMEM((tm + 8, D_MODEL), BF16),
            pltpu.VMEM((tm, D_MODEL), F32),
        ],
        compiler_params=pltpu.CompilerParams(
            dimension_semantics=("parallel", "arbitrary"), vmem_limit_bytes=VMEM_LIMIT),
        name="ffn",
    )(x1, x1, g, wup, wup, cw, cw, wd, gf)


def _swap_half(w):
    half = w.shape[-1] // 2
    return jnp.concatenate([-w[..., half:], w[..., :half]], axis=-1)


def _prep_inproj(w_in):
    o = 0
    parts = {}
    for name, n in (("qkv", 3 * D_MODEL), ("gate", D_MODEL), ("a", HEADS), ("b", HEADS), ("cq", MLA_Q_RANK),
                    ("ckv", MLA_KV_RANK), ("kpe", MLA_ROPE), ("gba", D_MODEL), ("gbb", D_MODEL)):
        parts[name] = w_in[:, o:o + n]
        o += n
    wz = jnp.concatenate([parts["qkv"], parts["gate"], parts["gba"], parts["gbb"], parts["cq"],
                          parts["kpe"], _swap_half(parts["kpe"]), parts["ckv"]], axis=1).astype(BF16)
    pad = jnp.zeros((D_MODEL, 128 - 3 * HEADS), w_in.dtype)
    wab = jnp.concatenate([parts["a"], parts["b"], parts["a"], pad], axis=1).astype(BF16)
    return wz, wab


def _lane_row(vec8, positions):
    row = jnp.zeros((1, 128), F32)
    for p in positions:
        row = row.at[0, p:p + HEADS].set(vec8.astype(F32))
    return row


def kernel(x, norm_mix_g, w_in, conv_qkv_w, gdn_a_log, gdn_dt_bias, gdn_norm_g, mla_q_norm_g, w_uq,
           mla_kv_norm_g, w_ukv, w_o_gdn, w_o_mla, w_out, norm_ffn_g, w_up, conv_ffn_w, w_down, norm_final_g):
    batch, seq, _ = x.shape
    t = batch * seq
    depth = w_in.shape[0]
    tm = min(1024, seq)
    x2 = x.reshape(t, D_MODEL)

    half = MLA_ROPE // 2
    inv = ROPE_THETA ** (-jnp.arange(half, dtype=F32) / half)
    ang = jnp.arange(seq, dtype=F32)[:, None] * inv[None, :]
    zpad = jnp.zeros((seq, 128 - MLA_ROPE), F32)
    cos = jnp.concatenate([jnp.cos(ang), jnp.cos(ang), zpad], axis=1)
    sin = jnp.concatenate([jnp.sin(ang), jnp.sin(ang), zpad], axis=1)

    y = None
    for l in range(depth):
        wz, wab = _prep_inproj(w_in[l])
        nega = _lane_row(-jnp.exp(gdn_a_log[l].astype(F32)), (0, 16))
        dtb = _lane_row(gdn_dt_bias[l], (0, 16))
        z, gab = _inproj(x2, norm_mix_g[l][None, :], wz, wab, nega, dtb, tm=tm, tn=768)

        cw = conv_qkv_w[l].reshape(GDN_CONV, 3, D_MODEL).transpose(1, 0, 2)
        oa = _gdn(z, gab, cw, gdn_norm_g[l][None, :], batch=batch, seq=seq)

        wq = w_uq[l].reshape(MLA_Q_RANK, HEADS, HEAD_DIM + MLA_ROPE)
        wq = jnp.concatenate([wq, _swap_half(wq[..., HEAD_DIM:])], axis=-1)
        wq = wq.reshape(MLA_Q_RANK, HEADS * QK_PAD).astype(BF16)
        q, k, v = _mlaproj(z, mla_q_norm_g[l][None, :], mla_kv_norm_g[l][None, :], wq, w_ukv[l].astype(BF16),
                           cos, sin, batch=batch, seq=seq, tm=min(512, seq))
        ob = _attn(q, k, v, tq=min(512, seq)).reshape(t, D_MODEL)

        x2 = _mix(oa, ob, z, x2, w_o_gdn[l].astype(BF16), w_o_mla[l].astype(BF16), w_out[l].astype(BF16),
                  tm=min(512, seq))
        y = _ffn(x2, norm_ffn_g[l][None, :], w_up[l].astype(BF16), conv_ffn_w[l], w_down[l].astype(BF16),
                 norm_final_g[None, :], seq=seq, tm=tm, tf=256, final_norm=(l == depth - 1))
        x2 = y
    return y.reshape(batch, seq, D_MODEL)
```
